```python
import math
import jax, jax.numpy as jnp
from jax import lax
import numpy as np

D_MODEL = 1024
BATCH = 8
SEQ = 2048
DEPTH = 1

ATTN_WIDTH = 512
HEAD_DIM = 64
N_DIFF_HEADS = ATTN_WIDTH // (2 * HEAD_DIM)
CONV_WIDTH = 512
CONV_KERNEL = 31
N_BUCKETS = 32
MAX_DISTANCE = 128
Q_BLOCK = 128
LN_EPS = 1e-5
DEEPNORM_ALPHA = (2.0 * DEPTH) ** 0.25
DEEPNORM_BETA = (8.0 * DEPTH) ** -0.25
D_IN = 4 * ATTN_WIDTH + 3 * CONV_WIDTH + 2 * D_MODEL
SPLITS = (ATTN_WIDTH, 2 * ATTN_WIDTH, 3 * ATTN_WIDTH, 4 * ATTN_WIDTH,
          4 * ATTN_WIDTH + 2 * CONV_WIDTH, 4 * ATTN_WIDTH + 3 * CONV_WIDTH)

kernel_name = "hybrid_diffattn_conformer_gated_deepnorm"


def layer_norm(x, g, b):
    xf = x.astype(jnp.float32)
    mu = xf.mean(-1, keepdims=True)
    var = jnp.square(xf - mu).mean(-1, keepdims=True)
    return ((xf - mu) * lax.rsqrt(var + LN_EPS) * g + b).astype(x.dtype)


def rms_norm(x, g):
    xf = x.astype(jnp.float32)
    return (xf * lax.rsqrt(jnp.mean(xf * xf, -1, keepdims=True) + LN_EPS) * g).astype(x.dtype)


def t5_causal_bucket(q_pos, k_pos):
    n = jnp.maximum(q_pos[:, None] - k_pos[None, :], 0)
    max_exact = N_BUCKETS // 2
    nf = jnp.maximum(n, 1).astype(jnp.float32)
    large = max_exact + (jnp.log(nf / max_exact) / math.log(MAX_DISTANCE / max_exact)
                         * (N_BUCKETS - max_exact)).astype(jnp.int32)
    large = jnp.minimum(large, N_BUCKETS - 1)
    return jnp.where(n < max_exact, n, large)


def diff_attention(q, k, v, lam, rel_bias):
    B, S = q.shape[0], q.shape[1]
    nblk = S // Q_BLOCK
    scale = HEAD_DIM ** -0.5
    qb = q.reshape(B, nblk, Q_BLOCK, N_DIFF_HEADS, 2, HEAD_DIM).transpose(1, 0, 2, 3, 4, 5)
    k_pos = jnp.arange(S)

    def block(args):
        i, q_i = args
        q_pos = i * Q_BLOCK + jnp.arange(Q_BLOCK)
        s = jnp.einsum('bqhcd,bkhcd->bhcqk', q_i, k).astype(jnp.float32) * scale
        bias = rel_bias.astype(jnp.float32)[t5_causal_bucket(q_pos, k_pos)]
        bias = bias.transpose(2, 0, 1)[None, :, None]
        mask = k_pos[None, :] <= q_pos[:, None]
        s = jnp.where(mask, s + bias, -1e30)
        p = jax.nn.softmax(s, axis=-1)
        p = p[:, :, 0] - lam * p[:, :, 1]
        return jnp.einsum('bhqk,bkhe->bqhe', p.astype(v.dtype), v)

    o = lax.map(block, (jnp.arange(nblk), qb))
    return o.transpose(1, 0, 2, 3, 4).reshape(B, S, N_DIFF_HEADS, 2 * HEAD_DIM)


def causal_depthwise_conv(u, w, b):
    y = lax.conv_general_dilated(u, w[:, None, :], window_strides=(1,),
                                 padding=[(CONV_KERNEL - 1, 0)],
                                 dimension_numbers=('NWC', 'WIO', 'NWC'),
                                 feature_group_count=u.shape[-1])
    return y + b


def setup_inputs(seed: int = 0) -> dict:
    key = jax.random.key(seed)
    ks = jax.random.split(key, 16)
    f32 = jnp.float32
    x = jax.random.normal(ks[0], (BATCH, SEQ, D_MODEL), f32)
    w_in = jax.random.normal(ks[1], (DEPTH, D_MODEL, D_IN), f32) * D_MODEL ** -0.5
    col_scale = jnp.ones((D_IN,), f32).at[2 * ATTN_WIDTH:3 * ATTN_WIDTH].set(DEEPNORM_BETA)
    w_in = w_in * col_scale
    lambda_qk = jax.random.normal(ks[2], (DEPTH, 4, HEAD_DIM), f32) * 0.1
    subln_w = 1.0 + 0.02 * jax.random.normal(ks[3], (DEPTH, 2 * HEAD_DIM), f32)
    w_attn_proj = jax.random.normal(ks[4], (DEPTH, ATTN_WIDTH, D_MODEL), f32) * ATTN_WIDTH ** -0.5 * DEEPNORM_BETA
    conv_w = jax.random.normal(ks[5], (DEPTH, CONV_KERNEL, CONV_WIDTH), f32) * CONV_KERNEL ** -0.5
    conv_b = 0.02 * jax.random.normal(ks[6], (DEPTH, CONV_WIDTH), f32)
    conv_ln_g = 1.0 + 0.02 * jax.random.normal(ks[7], (DEPTH, CONV_WIDTH), f32)
    conv_ln_b = 0.02 * jax.random.normal(ks[8], (DEPTH, CONV_WIDTH), f32)
    w_conv_proj = jax.random.normal(ks[9], (DEPTH, CONV_WIDTH, D_MODEL), f32) * CONV_WIDTH ** -0.5 * DEEPNORM_BETA
    b_conv_proj = 0.02 * jax.random.normal(ks[10], (DEPTH, D_MODEL), f32)
    w_out = jax.random.normal(ks[11], (DEPTH, D_MODEL, D_MODEL), f32) * D_MODEL ** -0.5 * DEEPNORM_BETA
    post_ln_g = 1.0 + 0.02 * jax.random.normal(ks[12], (DEPTH, D_MODEL), f32)
    post_ln_b = 0.02 * jax.random.normal(ks[13], (DEPTH, D_MODEL), f32)
    rel_bias = 0.5 * jax.random.normal(ks[14], (N_BUCKETS, N_DIFF_HEADS), f32)
    return {"x": x, "w_in": w_in, "lambda_qk": lambda_qk, "subln_w": subln_w,
            "w_attn_proj": w_attn_proj, "conv_w": conv_w, "conv_b": conv_b,
            "conv_ln_g": conv_ln_g, "conv_ln_b": conv_ln_b, "w_conv_proj": w_conv_proj,
            "b_conv_proj": b_conv_proj, "w_out": w_out, "post_ln_g": post_ln_g,
            "post_ln_b": post_ln_b, "rel_bias": rel_bias}


def reference(x, w_in, lambda_qk, subln_w, w_attn_proj, conv_w, conv_b, conv_ln_g, conv_ln_b,
              w_conv_proj, b_conv_proj, w_out, post_ln_g, post_ln_b, rel_bias):
    B, S, _ = x.shape
    h = x
    for l in range(DEPTH):
        lambda_init = 0.8 - 0.6 * math.exp(-0.3 * l)
        proj = h @ w_in[l]
        q, k, v, g_attn, glu, g_conv, gates = jnp.split(proj, SPLITS, axis=-1)

        lq = lambda_qk[l].astype(jnp.float32)
        lam = jnp.exp(jnp.sum(lq[0] * lq[1])) - jnp.exp(jnp.sum(lq[2] * lq[3])) + lambda_init
        q = q.reshape(B, S, N_DIFF_HEADS, 2, HEAD_DIM)
        k = k.reshape(B, S, N_DIFF_HEADS, 2, HEAD_DIM)
        v = v.reshape(B, S, N_DIFF_HEADS, 2 * HEAD_DIM)
        o = diff_attention(q, k, v, lam, rel_bias)
        o = rms_norm(o, subln_w[l]) * (1.0 - lambda_init)
        o = o.reshape(B, S, ATTN_WIDTH) * jax.nn.silu(g_attn)
        y_attn = o @ w_attn_proj[l]

        a, bgate = jnp.split(glu, 2, axis=-1)
        u = a * jax.nn.sigmoid(bgate)
        u = causal_depthwise_conv(u, conv_w[l], conv_b[l])
        u = layer_norm(u, conv_ln_g[l], conv_ln_b[l])
        u = jax.nn.silu(u) * jax.nn.silu(g_conv)
        y_conv = u @ w_conv_proj[l] + b_conv_proj[l]

        gate_attn, gate_conv = jnp.split(gates, 2, axis=-1)
        merged = jax.nn.sigmoid(gate_attn) * y_attn + jax.nn.sigmoid(gate_conv) * y_conv
        out = merged @ w_out[l]
        h = layer_norm(DEEPNORM_ALPHA * h + out, post_ln_g[l], post_ln_b[l])
    return h
```

```python
import functools
import math

import jax
import jax.numpy as jnp
from jax import lax
from jax.experimental import pallas as pl
from jax.experimental.pallas import tpu as pltpu

HEAD_DIM = 64
ATTN_WIDTH = 512
N_HEADS = ATTN_WIDTH // (2 * HEAD_DIM)
HEAD_WIDTH = 2 * HEAD_DIM
CONV_WIDTH = 512
CONV_KERNEL = 31
N_BUCKETS = 32
MAX_EXACT = N_BUCKETS // 2
MAX_DISTANCE = 128
LN_EPS = 1e-5
DEPTH = 1
DEEPNORM_ALPHA = (2.0 * DEPTH) ** 0.25
MASK_VALUE = -1e30

ATTN_TILE = 256
TAIL_TILE = 256
CONV_HALO = 32
VMEM_LIMIT_BYTES = 56 * 1024 * 1024


def _sigmoid(x):
    return 1.0 / (1.0 + jnp.exp(-x))


def _silu(x):
    return x * _sigmoid(x)


def _bias_tiles_kernel(table_ref, out_ref):
    t = out_ref.shape[-1]
    key_pos = lax.broadcasted_iota(jnp.int32, (t, t), 0)
    qry_pos = lax.broadcasted_iota(jnp.int32, (t, t), 1)
    for tile in range(2):
        n = jnp.maximum(qry_pos - key_pos + tile * t, 0)
        nf = jnp.maximum(n, 1).astype(jnp.float32)
        large = MAX_EXACT + (jnp.log(nf / MAX_EXACT) / math.log(MAX_DISTANCE / MAX_EXACT)
                             * (N_BUCKETS - MAX_EXACT)).astype(jnp.int32)
        large = jnp.minimum(large, N_BUCKETS - 1)
        bucket = jnp.where(n < MAX_EXACT, n, large)
        for h in range(N_HEADS):
            acc = jnp.zeros((t, t), jnp.float32)
            for b in range(N_BUCKETS):
                acc = jnp.where(bucket == b, table_ref[b, h], acc)
            out_ref[h, tile] = acc


def _bias_tiles(rel_bias, t):
    return pl.pallas_call(
        _bias_tiles_kernel,
        out_shape=jax.ShapeDtypeStruct((N_HEADS, 2, t, t), jnp.float32),
        in_specs=[pl.BlockSpec(memory_space=pltpu.SMEM)],
        out_specs=pl.BlockSpec(memory_space=pltpu.VMEM),
        name="bias_tiles",
    )(rel_bias)


def _qkv_kernel(x_ref, wk_ref, wqv_t_ref, qt_ref, k_ref, vt_ref):
    xh = x_ref[0].astype(jnp.bfloat16)
    k = jnp.dot(xh, wk_ref[...], preferred_element_type=jnp.float32)
    qv_t = lax.dot_general(wqv_t_ref[...], xh, (((1,), (1,)), ((), ())),
                           preferred_element_type=jnp.float32)
    scale = HEAD_DIM ** -0.5
    zeros = jnp.zeros((HEAD_DIM, xh.shape[0]), jnp.bfloat16)
    for h in range(N_HEADS):
        k_ref[0, h, 0] = k[:, h * HEAD_WIDTH:(h + 1) * HEAD_WIDTH].astype(jnp.bfloat16)
        vt_ref[0, h, 0] = qv_t[ATTN_WIDTH + h * HEAD_WIDTH:
                               ATTN_WIDTH + (h + 1) * HEAD_WIDTH].astype(jnp.bfloat16)
        for c in range(2):
            row = h * HEAD_WIDTH + c * HEAD_DIM
            q_c = (qv_t[row:row + HEAD_DIM] * scale).astype(jnp.bfloat16)
            qt_ref[0, h, c, 0, c * HEAD_DIM:(c + 1) * HEAD_DIM, :] = q_c
            qt_ref[0, h, c, 0, (1 - c) * HEAD_DIM:(2 - c) * HEAD_DIM, :] = zeros


def _qkv_proj(x, wk, wqv_t, t):
    bsz, seq, d = x.shape
    nt = seq // t
    return pl.pallas_call(
        _qkv_kernel,
        grid=(bsz, nt),
        in_specs=[
            pl.BlockSpec((1, t, d), lambda b, i: (b, i, 0)),
            pl.BlockSpec(wk.shape, lambda b, i: (0, 0)),
            pl.BlockSpec(wqv_t.shape, lambda b, i: (0, 0)),
        ],
        out_specs=[
            pl.BlockSpec((1, N_HEADS, 2, 1, HEAD_WIDTH, t), lambda b, i: (b, 0, 0, i, 0, 0)),
            pl.BlockSpec((1, N_HEADS, 1, t, HEAD_WIDTH), lambda b, i: (b, 0, i, 0, 0)),
            pl.BlockSpec((1, N_HEADS, 1, HEAD_WIDTH, t), lambda b, i: (b, 0, i, 0, 0)),
        ],
        out_shape=[
            jax.ShapeDtypeStruct((bsz, N_HEADS, 2, nt, HEAD_WIDTH, t), jnp.bfloat16),
            jax.ShapeDtypeStruct((bsz, N_HEADS, nt, t, HEAD_WIDTH), jnp.bfloat16),
            jax.ShapeDtypeStruct((bsz, N_HEADS, nt, HEAD_WIDTH, t), jnp.bfloat16),
        ],
        compiler_params=pltpu.CompilerParams(
            dimension_semantics=("arbitrary", "arbitrary"),
            vmem_limit_bytes=VMEM_LIMIT_BYTES),
        name="qkv_proj",
    )(x, wk, wqv_t)


def _attn_kernel(table_ref, lq_ref, qt_ref, k_ref, vt_ref, bias_ref, subw_ref, o_ref,
                 acc_ref, m_ref, l_ref, *, lambda_init):
    nt = k_ref.shape[2]
    t = k_ref.shape[3]
    h = pl.program_id(1)
    far_bias = table_ref[N_BUCKETS - 1, h]

    lq = lq_ref[...]
    lam = (jnp.exp(jnp.sum(lq[0:1] * lq[1:2], axis=-1, keepdims=True))
           - jnp.exp(jnp.sum(lq[2:3] * lq[3:4], axis=-1, keepdims=True)) + lambda_init)

    key_pos = lax.broadcasted_iota(jnp.int32, (t, t), 0)
    qry_pos = lax.broadcasted_iota(jnp.int32, (t, t), 1)
    causal = key_pos <= qry_pos

    def step(i, j, kind):
        k_blk = k_ref[0, 0, j]
        vt_blk = vt_ref[0, 0, j]
        for c in range(2):
            s = jnp.dot(k_blk, qt_ref[0, 0, c, i], preferred_element_type=jnp.float32)
            if kind == "far":
                s = s + far_bias
            elif kind == "sub":
                s = s + bias_ref[0, 1]
            else:
                s = jnp.where(causal, s + bias_ref[0, 0], MASK_VALUE)
            m_old = m_ref[c]
            m_new = jnp.maximum(m_old, jnp.max(s, axis=0, keepdims=True))
            alpha = jnp.exp(m_old - m_new)
            e = jnp.exp(s - m_new)
            l_ref[c] = alpha * l_ref[c] + jnp.sum(e, axis=0, keepdims=True)
            m_ref[c] = m_new
            pv = jnp.dot(vt_blk, e.astype(jnp.bfloat16), preferred_element_type=jnp.float32)
            acc_ref[c] = alpha * acc_ref[c] + pv

    def q_block(i, carry):
        m_ref[...] = jnp.full(m_ref.shape, -jnp.inf, jnp.float32)
        l_ref[...] = jnp.zeros(l_ref.shape, jnp.float32)
        acc_ref[...] = jnp.zeros(acc_ref.shape, jnp.float32)

        def far_body(j, c2):
            step(i, j, "far")
            return c2

        lax.fori_loop(0, jnp.maximum(i - 1, 0), far_body, 0)

        @pl.when(i >= 1)
        def _():
            step(i, i - 1, "sub")

        step(i, i, "diag")

        o_t = acc_ref[0] / l_ref[0] - lam * (acc_ref[1] / l_ref[1])
        ms = jnp.mean(o_t * o_t, axis=0, keepdims=True)
        o_t = o_t * lax.rsqrt(ms + LN_EPS)
        o = o_t.T * subw_ref[...] * (1.0 - lambda_init)
        o_ref[0, pl.ds(pl.multiple_of(i * t, t), t), :] = o
        return carry

    lax.fori_loop(0, nt, q_block, 0)


def _attention(rel_bias, lq, qt, k, vt, bias_tiles, subw, lambda_init):
    bsz, nh, _, nt, hw, t = qt.shape
    seq = nt * t
    kern = functools.partial(_attn_kernel, lambda_init=lambda_init)
    return pl.pallas_call(
        kern,
        grid=(bsz, nh),
        in_specs=[
            pl.BlockSpec(memory_space=pltpu.SMEM),
            pl.BlockSpec(lq.shape, lambda b, h: (0, 0)),
            pl.BlockSpec((1, 1, 2, nt, hw, t), lambda b, h: (b, h, 0, 0, 0, 0)),
            pl.BlockSpec((1, 1, nt, t, hw), lambda b, h: (b, h, 0, 0, 0)),
            pl.BlockSpec((1, 1, nt, hw, t), lambda b, h: (b, h, 0, 0, 0)),
            pl.BlockSpec((1, 2, t, t), lambda b, h: (h, 0, 0, 0)),
            pl.BlockSpec(subw.shape, lambda b, h: (0, 0)),
        ],
        out_specs=pl.BlockSpec((1, seq, hw), lambda b, h: (b, 0, h)),
        out_shape=jax.ShapeDtypeStruct((bsz, seq, nh * hw), jnp.float32),
        scratch_shapes=[
            pltpu.VMEM((2, hw, t), jnp.float32),
            pltpu.VMEM((2, 1, t), jnp.float32),
            pltpu.VMEM((2, 1, t), jnp.float32),
        ],
        compiler_params=pltpu.CompilerParams(
            dimension_semantics=("arbitrary", "arbitrary"),
            vmem_limit_bytes=VMEM_LIMIT_BYTES),
        name="diff_attention",
    )(rel_bias, lq, qt, k, vt, bias_tiles, subw)


def _layer_norm(x, g, b):
    mu = jnp.mean(x, axis=-1, keepdims=True)
    xc = x - mu
    var = jnp.mean(xc * xc, axis=-1, keepdims=True)
    return xc * lax.rsqrt(var + LN_EPS) * g + b


def _tail_kernel(x_ref, o_ref, wr_ref, wap_ref, cw_ref, cb_ref, clg_ref, clb_ref, wcp_ref,
                 bcp_ref, wout_ref, plg_ref, plb_ref, out_ref, ubuf_ref):
    t = x_ref.shape[1]
    d = x_ref.shape[2]
    cwid = CONV_WIDTH

    @pl.when(pl.program_id(1) == 0)
    def _():
        ubuf_ref[0:CONV_HALO, :] = jnp.zeros((CONV_HALO, cwid), jnp.float32)

    x = x_ref[0]
    xh = x.astype(jnp.bfloat16)

    def proj(lo, hi):
        return jnp.dot(xh, wr_ref[:, lo:hi], preferred_element_type=jnp.float32)

    u = proj(cwid, 2 * cwid) * _sigmoid(proj(2 * cwid, 3 * cwid))
    ubuf_ref[CONV_HALO:CONV_HALO + t, :] = u
    base = CONV_HALO - (CONV_KERNEL - 1)
    conv = jnp.zeros((t, cwid), jnp.float32) + cb_ref[...]
    for j in range(CONV_KERNEL):
        conv = conv + ubuf_ref[base + j:base + j + t, :] * cw_ref[j:j + 1, :]
    ubuf_ref[0:CONV_HALO, :] = ubuf_ref[t:t + CONV_HALO, :]
    uc = _silu(_layer_norm(conv, clg_ref[...], clb_ref[...])) * _silu(proj(3 * cwid, 4 * cwid))
    y_conv = jnp.dot(uc.astype(jnp.bfloat16), wcp_ref[...],
                     preferred_element_type=jnp.float32) + bcp_ref[...]

    og = o_ref[0] * _silu(proj(0, cwid))
    y_attn = jnp.dot(og.astype(jnp.bfloat16), wap_ref[...], preferred_element_type=jnp.float32)

    g0 = 4 * cwid
    merged = _sigmoid(proj(g0, g0 + d)) * y_attn + _sigmoid(proj(g0 + d, g0 + 2 * d)) * y_conv
    out = jnp.dot(merged.astype(jnp.bfloat16), wout_ref[...], preferred_element_type=jnp.float32)
    out_ref[0] = _layer_norm(DEEPNORM_ALPHA * x + out, plg_ref[...], plb_ref[...])


def _block_tail(x, o, wr, wap, cw, cb, clg, clb, wcp, bcp, wout, plg, plb, t):
    bsz, seq, d = x.shape
    nt = seq // t

    def full(a):
        return pl.BlockSpec(a.shape, lambda b, i: (0,) * a.ndim)

    return pl.pallas_call(
        _tail_kernel,
        grid=(bsz, nt),
        in_specs=[
            pl.BlockSpec((1, t, d), lambda b, i: (b, i, 0)),
            pl.BlockSpec((1, t, ATTN_WIDTH), lambda b, i: (b, i, 0)),
            full(wr), full(wap), full(cw), full(cb), full(clg), full(clb), full(wcp),
            full(bcp), full(wout), full(plg), full(plb),
        ],
        out_specs=pl.BlockSpec((1, t, d), lambda b, i: (b, i, 0)),
        out_shape=jax.ShapeDtypeStruct((bsz, seq, d), jnp.float32),
        scratch_shapes=[pltpu.VMEM((CONV_HALO + t, CONV_WIDTH), jnp.float32)],
        compiler_params=pltpu.CompilerParams(
            dimension_semantics=("arbitrary", "arbitrary"),
            vmem_limit_bytes=VMEM_LIMIT_BYTES),
        name="block_tail",
    )(x, o, wr, wap, cw, cb, clg, clb, wcp, bcp, wout, plg, plb)


def kernel(x, w_in, lambda_qk, subln_w, w_attn_proj, conv_w, conv_b, conv_ln_g, conv_ln_b,
           w_conv_proj, b_conv_proj, w_out, post_ln_g, post_ln_b, rel_bias):
    bsz, seq, d = x.shape
    assert w_in.shape[0] == DEPTH
    assert seq % ATTN_TILE == 0 and seq % TAIL_TILE == 0 and TAIL_TILE >= CONV_HALO
    assert ATTN_TILE >= MAX_DISTANCE
    aw = ATTN_WIDTH
    bf16 = jnp.bfloat16
    h = x
    bias_tiles = _bias_tiles(rel_bias, ATTN_TILE)
    for l in range(DEPTH):
        lambda_init = 0.8 - 0.6 * math.exp(-0.3 * l)
        w = w_in[l]
        wk = w[:, aw:2 * aw].astype(bf16)
        wqv_t = jnp.concatenate([w[:, 0:aw], w[:, 2 * aw:3 * aw]], axis=1).T.astype(bf16)
        wr = w[:, 3 * aw:].astype(bf16)

        qt, k, vt = _qkv_proj(h, wk, wqv_t, ATTN_TILE)
        o = _attention(rel_bias, lambda_qk[l], qt, k, vt, bias_tiles, subln_w[l][None, :],
                       lambda_init)
        h = _block_tail(h, o, wr, w_attn_proj[l].astype(bf16), conv_w[l], conv_b[l][None, :],
                        conv_ln_g[l][None, :], conv_ln_b[l][None, :],
                        w_conv_proj[l].astype(bf16), b_conv_proj[l][None, :],
                        w_out[l].astype(bf16), post_ln_g[l][None, :], post_ln_b[l][None, :],
                        TAIL_TILE)
    return h
```
